```python
import math
import jax
import jax.numpy as jnp
from jax import lax
import numpy as np

D_MODEL = 2048
BATCH = 8
SEQ = 2048
DEPTH = 2
DEC_BATCH = 2
DEC_SEQ = 8192
PAST_LEN = 128

N_BRANCH = 4
BR_W = D_MODEL // 2
RMS_EPS = 1e-6
MASK_NEG = -1e30
LB_FLOOR = 1e-30

SSD_W = BR_W
SSD_HEAD_DIM = 64
SSD_H = SSD_W // SSD_HEAD_DIM
SSD_G = 2
SSD_N = 128
SSD_CONV = 3
SSD_CHUNK = 128
SSD_CONV_CH = SSD_W + 2 * SSD_G * SSD_N

ATTN_HEAD_DIM = 64
ATTN_HQ = BR_W // ATTN_HEAD_DIM
ATTN_HKV = 4
ATTN_GROUP = ATTN_HQ // ATTN_HKV
ATTN_WINDOW = 128
ROPE_THETA = 10000.0

HG_W = BR_W
HG_HEAD_DIM = 128
HG_H = HG_W // HG_HEAD_DIM
HG_CHUNK = 32

HY_W = BR_W
HY_SHORT = 3
HY_BANDS = 16
HY_EMB = 1 + 2 * HY_BANDS
HY_FO = 64
HY_INNER = 2
HY_DECAY_TARGET = 1e-2
HY_FAST_DECAY = 0.3
HY_SLOW_DECAY = 1.5

N_EXPERTS = 16
EXPERT_FF = D_MODEL
EC_CAPACITY = 2

GATE_COLS = N_BRANCH * D_MODEL
SSD_COLS = 2 * SSD_W + 2 * SSD_G * SSD_N + 2 * SSD_H
ATTN_COLS = (ATTN_HQ + 2 * ATTN_HKV) * ATTN_HEAD_DIM
HG_COLS = 5 * HG_W
HY_COLS = 3 * HY_W
N_IN = GATE_COLS + SSD_COLS + ATTN_COLS + HG_COLS + HY_COLS
IN_SPLITS = (GATE_COLS, GATE_COLS + SSD_COLS, GATE_COLS + SSD_COLS + ATTN_COLS,
             GATE_COLS + SSD_COLS + ATTN_COLS + HG_COLS)

kernel_name = 'hybrid_bidir_encoder_ssd_swa_hgrn2_hyena_ec'

F32 = jnp.float32


def rmsnorm(x, g):
    xf = x.astype(F32)
    y = xf * lax.rsqrt(jnp.mean(xf * xf, axis=-1, keepdims=True) + RMS_EPS)
    return (y * g.astype(F32)).astype(x.dtype)


def _rev(t):
    return jnp.flip(t, axis=1)


def centred_dwconv(u, w, b):
    K = w.shape[0]
    pad = K // 2
    L = u.shape[1]
    up = jnp.pad(u, ((0, 0), (pad, pad), (0, 0)))
    return sum(up[:, j:j + L] * w[j] for j in range(K)) + b


def _masked_exp(diff, mask):
    return jnp.where(mask, jnp.exp(jnp.where(mask, diff, 0.0)), 0.0)


def ssd_scan(xdt_in, da_in, bm, cm):
    B, L, H, P = xdt_in.shape
    R = H // SSD_G
    C = SSD_CHUNK
    n = L // C

    def chunks(t):
        t = t.astype(F32).reshape((B, n, C) + t.shape[2:])
        return jnp.moveaxis(t, 1, 0)

    xs = (chunks(xdt_in.reshape(B, L, SSD_G, R, P)), chunks(da_in.reshape(B, L, SSD_G, R)),
          chunks(bm), chunks(cm))
    causal = jnp.tril(jnp.ones((C, C), bool))[None, :, :, None, None]

    def step(S, inp):
        xc, ac, bc, cc = inp
        cs = jnp.cumsum(ac, axis=1)
        seg = _masked_exp(cs[:, :, None] - cs[:, None, :], causal)
        cb = jnp.einsum('bign,bjgn->bijg', cc, bc)
        y = jnp.einsum('bijg,bijgr,bjgrp->bigrp', cb, seg, xc)
        y = y + jnp.einsum('bign,bgrpn->bigrp', cc, S) * jnp.exp(cs)[..., None]
        last = cs[:, -1]
        w = jnp.exp(last[:, None] - cs)
        S = jnp.exp(last)[..., None, None] * S + jnp.einsum('bjgn,bjgr,bjgrp->bgrpn', bc, w, xc)
        return S, y

    S0 = jnp.zeros((B, SSD_G, R, P, SSD_N), F32)
    _, ys = lax.scan(step, S0, xs)
    return jnp.moveaxis(ys, 0, 1).reshape(B, L, H, P)


def ssd_mixer(seg, conv_w, conv_b, dt_bias, a_log, d_skip, norm_w):
    B, L, _ = seg.shape
    z, xbc, dt_raw = jnp.split(seg, [SSD_W, SSD_W + SSD_CONV_CH], axis=-1)
    xbc = jax.nn.silu(centred_dwconv(xbc, conv_w, conv_b))
    xs, bm, cm = jnp.split(xbc, [SSD_W, SSD_W + SSD_G * SSD_N], axis=-1)
    xs = xs.reshape(B, L, SSD_H, SSD_HEAD_DIM).astype(F32)
    bm = bm.reshape(B, L, SSD_G, SSD_N)
    cm = cm.reshape(B, L, SSD_G, SSD_N)
    dt = jax.nn.softplus(dt_raw.reshape(B, L, 2, SSD_H).astype(F32) + dt_bias.astype(F32))
    a = -jnp.exp(a_log.astype(F32))
    dt_f, dt_b = dt[:, :, 0], dt[:, :, 1]
    y_f = ssd_scan(xs * dt_f[..., None], dt_f * a[0], bm, cm)
    y_b = _rev(ssd_scan(_rev(xs * dt_b[..., None]), _rev(dt_b * a[1]), _rev(bm), _rev(cm)))
    y = y_f + y_b + xs * d_skip.astype(F32)[:, None]
    y = y.reshape(B, L, SSD_W) * jax.nn.silu(z.astype(F32))
    y = rmsnorm(y.reshape(B, L, SSD_G, SSD_W // SSD_G), norm_w.reshape(SSD_G, SSD_W // SSD_G))
    return y.reshape(B, L, SSD_W)


def rope(t, pos):
    half = ATTN_HEAD_DIM // 2
    inv = ROPE_THETA ** (-jnp.arange(half, dtype=F32) / half)
    ang = pos[:, None] * inv[None]
    cos = jnp.cos(ang)[None, :, None]
    sin = jnp.sin(ang)[None, :, None]
    t = t.astype(F32)
    t1, t2 = t[..., :half], t[..., half:]
    return jnp.concatenate([t1 * cos - t2 * sin, t2 * cos + t1 * sin], axis=-1)


def banded_attention(q, k, v, sink):
    B, L, _, Dh = q.shape
    W = ATTN_WINDOW
    nb = L // W
    qb = q.reshape(B, nb, W, ATTN_HKV, ATTN_GROUP, Dh)

    def band(t):
        tp = jnp.pad(t.astype(F32), ((0, 0), (W, W), (0, 0), (0, 0))).reshape(B, nb + 2, W, ATTN_HKV, Dh)
        return jnp.concatenate([tp[:, :-2], tp[:, 1:-1], tp[:, 2:]], axis=2)

    kb, vb = band(k), band(v)
    s = jnp.einsum('bnqhgd,bnkhd->bnhgqk', qb, kb) * (1.0 / math.sqrt(Dh))
    qpos = jnp.arange(nb)[:, None] * W + jnp.arange(W)[None]
    kpos = jnp.arange(nb)[:, None] * W - W + jnp.arange(3 * W)[None]
    valid = ((jnp.abs(qpos[:, :, None] - kpos[:, None, :]) <= W)
             & (kpos >= 0)[:, None, :] & (kpos < L)[:, None, :])
    valid = valid[None, :, None, None]
    s = jnp.where(valid, s, MASK_NEG)
    sk = sink.astype(F32).reshape(ATTN_HKV, ATTN_GROUP)[None, None, :, :, None, None]
    m = jnp.maximum(jnp.max(s, axis=-1, keepdims=True), sk)
    p = jnp.where(valid, jnp.exp(s - m), 0.0)
    denom = jnp.sum(p, axis=-1, keepdims=True) + jnp.exp(sk - m)
    o = jnp.einsum('bnhgqk,bnkhd->bnqhgd', p / denom, vb)
    return o.reshape(B, L, ATTN_HQ * Dh)


def attention_mixer(seg, sink):
    B, L, _ = seg.shape
    q, k, v = jnp.split(seg, [ATTN_HQ * ATTN_HEAD_DIM, (ATTN_HQ + ATTN_HKV) * ATTN_HEAD_DIM], axis=-1)
    q = q.reshape(B, L, ATTN_HQ, ATTN_HEAD_DIM)
    k = k.reshape(B, L, ATTN_HKV, ATTN_HEAD_DIM)
    v = v.reshape(B, L, ATTN_HKV, ATTN_HEAD_DIM)
    pos = jnp.arange(L, dtype=F32)
    return banded_attention(rope(q, pos), rope(k, pos), v, sink)


def hgrn2_scan(q, lf, k, v):
    B, L, H, K = q.shape
    V = v.shape[-1]
    C = HG_CHUNK
    n = L // C

    def chunks(t):
        return t.reshape(B, n, C, H, t.shape[-1]).transpose(1, 0, 3, 2, 4)

    causal = jnp.tril(jnp.ones((C, C), bool))[:, :, None]

    def step(S, inp):
        qc, lfc, kc, vc = inp
        b = jnp.cumsum(lfc, axis=2)
        dec = _masked_exp(b[:, :, :, None] - b[:, :, None], causal)
        att = jnp.einsum('bhik,bhijk,bhjk->bhij', qc, dec, kc)
        o = jnp.einsum('bhij,bhjv->bhiv', att, vc) + jnp.einsum('bhik,bhkv->bhiv', qc * jnp.exp(b), S)
        last = b[:, :, -1]
        S = jnp.exp(last)[..., None] * S + jnp.einsum('bhjk,bhjv->bhkv', kc * jnp.exp(last[:, :, None] - b), vc)
        return S, o

    S0 = jnp.zeros((B, H, K, V), F32)
    _, os = lax.scan(step, S0, (chunks(q), chunks(lf), chunks(k), chunks(v)))
    return os.transpose(1, 0, 3, 2, 4).reshape(B, L, H, V)


def hgrn2_mixer(seg, lb_f, lb_b, norm_w):
    B, L, _ = seg.shape
    q, ff, fb, i, g = jnp.split(seg, 5, axis=-1)
    shp = (B, L, HG_H, HG_HEAD_DIM)
    q, ff, fb, i = (t.reshape(shp).astype(F32) for t in (q, ff, fb, i))

    def gates(raw, lb):
        lb = lb.astype(F32).reshape(HG_H, HG_HEAD_DIM)
        lf = jnp.logaddexp(jnp.log(jnp.maximum(lb, LB_FLOOR)), jnp.log1p(-lb) + jax.nn.log_sigmoid(raw))
        return lf, -jnp.expm1(lf)

    lf_f, k_f = gates(ff, lb_f)
    lf_b, k_b = gates(fb, lb_b)
    o = hgrn2_scan(q, lf_f, k_f, i) + _rev(hgrn2_scan(_rev(q), _rev(lf_b), _rev(k_b), _rev(i)))
    o = rmsnorm(o, norm_w) * jax.nn.silu(g.reshape(shp).astype(F32))
    return o.reshape(B, L, HG_W)


def hyena_filters(L, w_in, b_in, w_mid, b_mid, freq, w_out):
    t = jnp.linspace(0.0, 1.0, L, dtype=F32)[:, None]
    w = 2.0 * math.pi * jnp.arange(L, dtype=F32)[:, None] / L
    bands = jnp.linspace(1e-4, HY_BANDS - 1, HY_BANDS, dtype=F32)[None]
    z = jnp.concatenate([t, jnp.cos(bands * w), -jnp.sin(bands * w)], axis=-1)
    h = jnp.sin(freq[0] * (z @ w_in + b_in))
    for m in range(HY_INNER):
        h = jnp.sin(freq[m + 1] * (h @ w_mid[m] + b_mid[m]))
    h = (h @ w_out).astype(F32)
    max_decay = math.log(HY_DECAY_TARGET) / HY_FAST_DECAY
    min_decay = math.log(HY_DECAY_TARGET) / HY_SLOW_DECAY
    deltas = jnp.abs(jnp.linspace(min_decay, max_decay, HY_W, dtype=F32))
    h = h * jnp.exp(-t * jnp.tile(deltas, 2)[None])
    return h[:, :HY_W], h[:, HY_W:]


def two_sided_fftconv(u, h_fwd, h_bwd, bias):
    L = u.shape[1]
    zero = jnp.zeros((1, h_fwd.shape[1]), F32)
    buf = jnp.concatenate([h_fwd[:1] + h_bwd[:1], h_fwd[1:], zero, jnp.flip(h_bwd[1:], axis=0)], axis=0)
    hf = jnp.fft.rfft(buf, axis=0)
    uf = jnp.fft.rfft(u.astype(F32), n=2 * L, axis=1)
    y = jnp.fft.irfft(uf * hf[None], n=2 * L, axis=1)[:, :L]
    return y + u.astype(F32) * bias.astype(F32)


def hyena_mixer(seg, conv_w, conv_b, f_w_in, f_b_in, f_w_mid, f_b_mid, f_freq, f_w_out, bias):
    L = seg.shape[1]
    u = centred_dwconv(seg, conv_w, conv_b)
    x0, x1, v = jnp.split(u, 3, axis=-1)
    h_fwd, h_bwd = hyena_filters(L, f_w_in, f_b_in, f_w_mid, f_b_mid, f_freq, f_w_out)
    return two_sided_fftconv(v * x1, h_fwd, h_bwd, bias) * x0.astype(F32)


def expert_choice_ffn(h, w_router, w_gate, w_up, w_down):
    B, L, D = h.shape
    T = B * L
    t = h.reshape(T, D)
    aff = jax.nn.softmax((t @ w_router).astype(F32), axis=-1)
    cap = EC_CAPACITY * T // N_EXPERTS
    g, idx = lax.top_k(aff.T, cap)
    xe = t[idx]
    hid = jax.nn.silu(jnp.einsum('ecd,edf->ecf', xe, w_gate)) * jnp.einsum('ecd,edf->ecf', xe, w_up)
    ye = jnp.einsum('ecf,efd->ecd', hid, w_down) * g[..., None]
    out = jnp.zeros((T, D), ye.dtype).at[idx.reshape(-1)].add(ye.reshape(-1, D))
    return out.reshape(B, L, D)


def trunk(x, p):
    B, L, _ = x.shape
    lbw = jax.nn.softmax(p['hgrn_lb_logits'].astype(F32), axis=1)
    lb_all = jnp.cumsum(lbw, axis=1) - lbw[:, :1]
    for l in range(DEPTH):
        h = rmsnorm(x, p['norm_mix'][l])
        proj = h @ p['w_in'][l]
        gate, s_ssd, s_attn, s_hg, s_hy = jnp.split(proj, IN_SPLITS, axis=-1)
        y_a = ssd_mixer(s_ssd, p['ssd_conv_w'][l], p['ssd_conv_b'][l], p['ssd_dt_bias'][l],
                        p['ssd_a_log'][l], p['ssd_d'][l], p['ssd_norm'][l])
        y_b = attention_mixer(s_attn, p['attn_sink'][l])
        y_c = hgrn2_mixer(s_hg, lb_all[0, l], lb_all[1, l], p['hgrn_norm'][l])
        y_d = hyena_mixer(s_hy, p['hy_conv_w'][l], p['hy_conv_b'][l], p['hy_filt_w_in'][l],
                          p['hy_filt_b_in'][l], p['hy_filt_w_mid'][l], p['hy_filt_b_mid'][l],
                          p['hy_filt_freq'][l], p['hy_filt_w_out'][l], p['hy_bias'][l])
        ys = jnp.stack([y_a.astype(F32), y_b.astype(F32), y_c.astype(F32), y_d.astype(F32)], axis=0)
        branches = jnp.einsum('kblw,kwd->kbld', ys, p['w_branch'][l])
        gates = jax.nn.sigmoid(gate.reshape(B, L, N_BRANCH, D_MODEL).astype(F32))
        merged = jnp.einsum('blkd,kbld->bld', gates, branches)
        x = x + merged @ p['w_out'][l]
        x = x + expert_choice_ffn(rmsnorm(x, p['norm_ffn'][l]), p['w_router'][l],
                                  p['w_gate'][l], p['w_up'][l], p['w_down'][l])
    return rmsnorm(x, p['norm_final'])


def setup_inputs(seed: int = 0) -> dict:
    key = jax.random.key(seed)
    ks = iter(jax.random.split(key, 40))

    def nrm(shape, scale):
        return jax.random.normal(next(ks), shape, F32) * scale

    def gain(shape):
        return 1.0 + nrm(shape, 0.1)

    x_prompt = nrm((BATCH, SEQ, D_MODEL), 1.0)
    x_sample = nrm((DEC_BATCH, DEC_SEQ, D_MODEL), 1.0)
    norm_mix = gain((DEPTH, D_MODEL))
    w_in = nrm((DEPTH, D_MODEL, N_IN), D_MODEL ** -0.5)
    ssd_conv_w = nrm((DEPTH, SSD_CONV, SSD_CONV_CH), SSD_CONV ** -0.5)
    ssd_conv_b = nrm((DEPTH, SSD_CONV_CH), 0.02)
    dt0 = jnp.exp(jax.random.uniform(next(ks), (DEPTH, 2, SSD_H), F32, math.log(1e-3), math.log(1e-1)))
    ssd_dt_bias = dt0 + jnp.log(-jnp.expm1(-dt0))
    ssd_a_log = jnp.log(jax.random.uniform(next(ks), (DEPTH, 2, SSD_H), F32, 1.0, 16.0))
    ssd_d = gain((DEPTH, SSD_H))
    ssd_norm = gain((DEPTH, SSD_W))
    attn_sink = nrm((DEPTH, ATTN_HQ), 0.5)
    hgrn_lb_logits = nrm((2, DEPTH, HG_W), 1.0)
    hgrn_norm = gain((DEPTH, HG_HEAD_DIM))
    hy_conv_w = nrm((DEPTH, HY_SHORT, 3 * HY_W), HY_SHORT ** -0.5)
    hy_conv_b = nrm((DEPTH, 3 * HY_W), 0.02)
    hy_filt_w_in = nrm((DEPTH, HY_EMB, HY_FO), HY_EMB ** -0.5)
    hy_filt_b_in = nrm((DEPTH, HY_FO), 0.1)
    hy_filt_w_mid = nrm((DEPTH, HY_INNER, HY_FO, HY_FO), HY_FO ** -0.5)
    hy_filt_b_mid = nrm((DEPTH, HY_INNER, HY_FO), 0.1)
    hy_filt_freq = gain((DEPTH, HY_INNER + 1, HY_FO))
    hy_filt_w_out = nrm((DEPTH, HY_FO, 2 * HY_W), 0.02)
    hy_bias = nrm((DEPTH, HY_W), 0.5)
    w_branch = nrm((DEPTH, N_BRANCH, BR_W, D_MODEL), BR_W ** -0.5)
    w_out = nrm((DEPTH, D_MODEL, D_MODEL), D_MODEL ** -0.5)
    norm_ffn = gain((DEPTH, D_MODEL))
    w_router = nrm((DEPTH, D_MODEL, N_EXPERTS), D_MODEL ** -0.5)
    w_gate = nrm((DEPTH, N_EXPERTS, D_MODEL, EXPERT_FF), D_MODEL ** -0.5)
    w_up = nrm((DEPTH, N_EXPERTS, D_MODEL, EXPERT_FF), D_MODEL ** -0.5)
    w_down = nrm((DEPTH, N_EXPERTS, EXPERT_FF, D_MODEL), EXPERT_FF ** -0.5)
    norm_final = gain((D_MODEL,))
    return {'x_prompt': x_prompt, 'x_sample': x_sample, 'norm_mix': norm_mix, 'w_in': w_in,
            'ssd_conv_w': ssd_conv_w, 'ssd_conv_b': ssd_conv_b, 'ssd_dt_bias': ssd_dt_bias,
            'ssd_a_log': ssd_a_log, 'ssd_d': ssd_d, 'ssd_norm': ssd_norm, 'attn_sink': attn_sink,
            'hgrn_lb_logits': hgrn_lb_logits, 'hgrn_norm': hgrn_norm, 'hy_conv_w': hy_conv_w,
            'hy_conv_b': hy_conv_b, 'hy_filt_w_in': hy_filt_w_in, 'hy_filt_b_in': hy_filt_b_in,
            'hy_filt_w_mid': hy_filt_w_mid, 'hy_filt_b_mid': hy_filt_b_mid, 'hy_filt_freq': hy_filt_freq,
            'hy_filt_w_out': hy_filt_w_out, 'hy_bias': hy_bias, 'w_branch': w_branch, 'w_out': w_out,
            'norm_ffn': norm_ffn, 'w_router': w_router, 'w_gate': w_gate, 'w_up': w_up,
            'w_down': w_down, 'norm_final': norm_final}


def reference(x_prompt, x_sample, norm_mix, w_in, ssd_conv_w, ssd_conv_b, ssd_dt_bias, ssd_a_log,
              ssd_d, ssd_norm, attn_sink, hgrn_lb_logits, hgrn_norm, hy_conv_w, hy_conv_b,
              hy_filt_w_in, hy_filt_b_in, hy_filt_w_mid, hy_filt_b_mid, hy_filt_freq, hy_filt_w_out,
              hy_bias, w_branch, w_out, norm_ffn, w_router, w_gate, w_up, w_down, norm_final):
    p = {'norm_mix': norm_mix, 'w_in': w_in, 'ssd_conv_w': ssd_conv_w, 'ssd_conv_b': ssd_conv_b,
         'ssd_dt_bias': ssd_dt_bias, 'ssd_a_log': ssd_a_log, 'ssd_d': ssd_d, 'ssd_norm': ssd_norm,
         'attn_sink': attn_sink, 'hgrn_lb_logits': hgrn_lb_logits, 'hgrn_norm': hgrn_norm,
         'hy_conv_w': hy_conv_w, 'hy_conv_b': hy_conv_b, 'hy_filt_w_in': hy_filt_w_in,
         'hy_filt_b_in': hy_filt_b_in, 'hy_filt_w_mid': hy_filt_w_mid, 'hy_filt_b_mid': hy_filt_b_mid,
         'hy_filt_freq': hy_filt_freq, 'hy_filt_w_out': hy_filt_w_out, 'hy_bias': hy_bias,
         'w_branch': w_branch, 'w_out': w_out, 'norm_ffn': norm_ffn, 'w_router': w_router,
         'w_gate': w_gate, 'w_up': w_up, 'w_down': w_down, 'norm_final': norm_final}
    y_prompt = trunk(x_prompt, p)
    y_sample = trunk(x_sample, p)
    return (y_prompt, y_sample)
```

```python
import functools
import math

import jax
import jax.numpy as jnp
from jax import lax
from jax.experimental import pallas as pl
from jax.experimental.pallas import tpu as pltpu

F32 = jnp.float32
BF16 = jnp.bfloat16

D_MODEL = 2048
DEPTH = 2
N_BRANCH = 4
BR_W = D_MODEL // 2
RMS_EPS = 1e-6
MASK_NEG = -1e30
LB_FLOOR = 1e-30

SSD_W = BR_W
SSD_HEAD_DIM = 64
SSD_H = SSD_W // SSD_HEAD_DIM
SSD_G = 2
SSD_N = 128
SSD_CONV = 3
SSD_CHUNK = 128
SSD_CONV_CH = SSD_W + 2 * SSD_G * SSD_N

ATTN_HEAD_DIM = 64
ATTN_HQ = BR_W // ATTN_HEAD_DIM
ATTN_HKV = 4
ATTN_GROUP = ATTN_HQ // ATTN_HKV
ATTN_WINDOW = 128
ROPE_THETA = 10000.0

HG_W = BR_W
HG_HEAD_DIM = 128
HG_H = HG_W // HG_HEAD_DIM
HG_CHUNK = 32

HY_W = BR_W
HY_SHORT = 3
HY_BANDS = 16
HY_EMB = 1 + 2 * HY_BANDS
HY_FO = 64
HY_INNER = 2
HY_DECAY_TARGET = 1e-2
HY_FAST_DECAY = 0.3
HY_SLOW_DECAY = 1.5

N_EXPERTS = 16
EXPERT_FF = D_MODEL
EC_CAPACITY = 2

GATE_COLS = N_BRANCH * D_MODEL
SSD_COLS = 2 * SSD_W + 2 * SSD_G * SSD_N + 2 * SSD_H
ATTN_COLS = (ATTN_HQ + 2 * ATTN_HKV) * ATTN_HEAD_DIM
HG_COLS = 5 * HG_W
HY_COLS = 3 * HY_W
N_IN = GATE_COLS + SSD_COLS + ATTN_COLS + HG_COLS + HY_COLS

LANES = 128
VMEM_LIMIT = 56 * 1024 * 1024

OFF_GATE = 0
OFF_HG = OFF_GATE + GATE_COLS
OFF_HY = OFF_HG + HG_COLS
OFF_Z = OFF_HY + HY_COLS
OFF_Q = OFF_Z + SSD_W
OFF_XBC = OFF_Q + ATTN_HQ * ATTN_HEAD_DIM
OFF_KV = OFF_XBC + SSD_CONV_CH
OFF_DT = OFF_KV + 2 * ATTN_HKV * ATTN_HEAD_DIM
DT_PAD = LANES
PROJ_TN = 512
N_PROJ = -(-(OFF_DT + DT_PAD) // PROJ_TN) * PROJ_TN


def _params(sem, vmem=VMEM_LIMIT):
    return pltpu.CompilerParams(dimension_semantics=sem, vmem_limit_bytes=vmem)


def _norm_proj_kernel(x_ref, g_ref, w_ref, o_ref, h_ref):
    @pl.when(pl.program_id(1) == 0)
    def _():
        x = x_ref[...]
        ms = jnp.mean(x * x, axis=-1, keepdims=True)
        h_ref[...] = ((x * lax.rsqrt(ms + RMS_EPS)) * g_ref[...]).astype(BF16)

    o_ref[...] = jnp.dot(h_ref[...], w_ref[...], preferred_element_type=F32)


def norm_proj(x, g, w, tm=1024, tn=PROJ_TN):
    T, K = x.shape
    N = w.shape[1]
    tm = min(tm, T)
    return pl.pallas_call(
        _norm_proj_kernel,
        grid=(T // tm, N // tn),
        in_specs=[pl.BlockSpec((tm, K), lambda i, j: (i, 0)),
                  pl.BlockSpec((1, K), lambda i, j: (0, 0)),
                  pl.BlockSpec((K, tn), lambda i, j: (0, j))],
        out_specs=pl.BlockSpec((tm, tn), lambda i, j: (i, j)),
        out_shape=jax.ShapeDtypeStruct((T, N), F32),
        scratch_shapes=[pltpu.VMEM((tm, K), BF16)],
        compiler_params=_params(("arbitrary", "arbitrary")),
        name="norm_proj",
    )(x, g, w)


def _merge_kernel(ya_ref, yb_ref, yc_ref, yd_ref, wb_ref, g0_ref, g1_ref, g2_ref, g3_ref, o_ref):
    acc = None
    for k, (y_ref, g_ref) in enumerate(((ya_ref, g0_ref), (yb_ref, g1_ref), (yc_ref, g2_ref), (yd_ref, g3_ref))):
        br = jnp.dot(y_ref[...].astype(BF16), wb_ref[k], preferred_element_type=F32)
        t = jax.nn.sigmoid(g_ref[...]) * br
        acc = t if acc is None else acc + t
    o_ref[...] = acc.astype(BF16)


def merge_branches(ys, wb, proj, tm=512, tn=512):
    T = proj.shape[0]
    tm = min(tm, T)
    nj = D_MODEL // tn
    y_spec = pl.BlockSpec((tm, BR_W), lambda i, j: (i, 0))
    gate_specs = [pl.BlockSpec((tm, tn), functools.partial(lambda i, j, k: (i, (OFF_GATE // tn) + k * nj + j), k=k))
                  for k in range(N_BRANCH)]
    return pl.pallas_call(
        _merge_kernel,
        grid=(T // tm, nj),
        in_specs=[y_spec] * 4 + [pl.BlockSpec((N_BRANCH, BR_W, tn), lambda i, j: (0, 0, j))] + gate_specs,
        out_specs=pl.BlockSpec((tm, tn), lambda i, j: (i, j)),
        out_shape=jax.ShapeDtypeStruct((T, D_MODEL), BF16),
        compiler_params=_params(("arbitrary", "arbitrary")),
        name="merge_branches",
    )(*ys, wb, proj, proj, proj, proj)


def _mm_res_kernel(m_ref, w_ref, x_ref, o_ref):
    o_ref[...] = x_ref[...] + jnp.dot(m_ref[...], w_ref[...], preferred_element_type=F32)


def mm_residual(m, w, x, tm=1024, tn=512):
    T, K = m.shape
    N = w.shape[1]
    tm = min(tm, T)
    return pl.pallas_call(
        _mm_res_kernel,
        grid=(T // tm, N // tn),
        in_specs=[pl.BlockSpec((tm, K), lambda i, j: (i, 0)),
                  pl.BlockSpec((K, tn), lambda i, j: (0, j)),
                  pl.BlockSpec((tm, tn), lambda i, j: (i, j))],
        out_specs=pl.BlockSpec((tm, tn), lambda i, j: (i, j)),
        out_shape=jax.ShapeDtypeStruct((T, N), F32),
        compiler_params=_params(("arbitrary", "arbitrary")),
        name="mm_residual",
    )(m, w, x)


def _final_norm_kernel(x_ref, g_ref, o_ref):
    x = x_ref[...]
    ms = jnp.mean(x * x, axis=-1, keepdims=True)
    o_ref[...] = (x * lax.rsqrt(ms + RMS_EPS)) * g_ref[...]


def final_norm(x, g, tm=1024):
    T, K = x.shape
    tm = min(tm, T)
    return pl.pallas_call(
        _final_norm_kernel,
        grid=(T // tm,),
        in_specs=[pl.BlockSpec((tm, K), lambda i: (i, 0)), pl.BlockSpec((1, K), lambda i: (0, 0))],
        out_specs=pl.BlockSpec((tm, K), lambda i: (i, 0)),
        out_shape=jax.ShapeDtypeStruct((T, K), F32),
        compiler_params=_params(("arbitrary",)),
        name="final_norm",
    )(x, g)


def _router_kernel(x_ref, g_ref, wr_ref, hx_ref, afft_ref):
    x = x_ref[...]
    ms = jnp.mean(x * x, axis=-1, keepdims=True)
    h = (x * lax.rsqrt(ms + RMS_EPS)) * g_ref[...]
    logits = jnp.dot(h.astype(BF16), wr_ref[...], preferred_element_type=F32)
    valid = lax.broadcasted_iota(jnp.int32, logits.shape, 1) < N_EXPERTS
    lm = jnp.where(valid, logits, MASK_NEG)
    m = jnp.max(lm, axis=-1, keepdims=True)
    p = jnp.where(valid, jnp.exp(lm - m), 0.0)
    aff = p / jnp.sum(p, axis=-1, keepdims=True)
    hx_ref[:, :D_MODEL] = h
    hx_ref[:, D_MODEL:] = aff
    afft_ref[...] = aff.T


def router(x, g, wr, tm=512):
    T, K = x.shape
    tm = min(tm, T)
    return pl.pallas_call(
        _router_kernel,
        grid=(T // tm,),
        in_specs=[pl.BlockSpec((tm, K), lambda i: (i, 0)),
                  pl.BlockSpec((1, K), lambda i: (0, 0)),
                  pl.BlockSpec((K, LANES), lambda i: (0, 0))],
        out_specs=[pl.BlockSpec((tm, K + LANES), lambda i: (i, 0)),
                   pl.BlockSpec((LANES, tm), lambda i: (0, i))],
        out_shape=[jax.ShapeDtypeStruct((T, K + LANES), F32), jax.ShapeDtypeStruct((LANES, T), F32)],
        compiler_params=_params(("arbitrary",)),
        name="router",
    )(x, g, wr)


TOPK_SG = 64


def _topk_kernel(afft_ref, idx_ref, bits_s, incl_s, row_s, *, cap, T):
    E = N_EXPERTS
    e = pl.program_id(0)
    nt = T // LANES

    @pl.when(e == 0)
    def _():
        bits_s[...] = lax.bitcast_convert_type(afft_ref[...], jnp.int32)
        capf = jnp.float32(cap)

        def search(i, p):
            c = p | lax.shift_left(jnp.int32(1), 30 - i)
            cnt = jnp.sum(jnp.where(bits_s[...] >= c, 1.0, 0.0), axis=1, keepdims=True)
            return jnp.where(cnt >= capf, c, p)

        thr = lax.fori_loop(0, 31, search, jnp.zeros((E, 1), jnp.int32))
        n_gt = jnp.sum(jnp.where(bits_s[...] > thr, 1.0, 0.0), axis=1, keepdims=True)
        need = capf - n_gt
        upper = (lax.broadcasted_iota(jnp.int32, (LANES, LANES), 0)
                 <= lax.broadcasted_iota(jnp.int32, (LANES, LANES), 1))
        upper = jnp.where(upper, 1.0, 0.0).astype(BF16)

        def prefix(j, carry):
            cg, ct = carry
            off = pl.multiple_of(j * LANES, LANES)
            b = bits_s[:, pl.ds(off, LANES)]
            gt = jnp.where(b > thr, 1.0, 0.0).astype(BF16)
            tie = jnp.where(b == thr, 1.0, 0.0).astype(BF16)
            gi = jnp.dot(gt, upper, preferred_element_type=F32) + cg
            ti = jnp.dot(tie, upper, preferred_element_type=F32) + ct
            incl_s[:, pl.ds(off, LANES)] = gi + jnp.minimum(ti, need)
            return gi[:, LANES - 1:LANES], ti[:, LANES - 1:LANES]

        zero = jnp.zeros((E, 1), F32)
        lax.fori_loop(0, nt, prefix, (zero, zero))

    erow = lax.broadcasted_iota(jnp.int32, (E, T), 0) == e
    row_s[...] = jnp.sum(jnp.where(erow, incl_s[...], 0.0), axis=0, keepdims=True)
    sub = lax.broadcasted_iota(jnp.int32, (TOPK_SG, LANES), 0)

    def group(g, _):
        base = pl.multiple_of(g * TOPK_SG, TOPK_SG)
        slot = (base + sub).astype(F32)

        def count(j, acc):
            off = pl.multiple_of(j * LANES, LANES)
            r = row_s[:, pl.ds(off, LANES)]
            return acc + jnp.where(r <= slot, 1.0, 0.0)

        acc = lax.fori_loop(0, nt, count, jnp.zeros((TOPK_SG, LANES), F32))
        tok = jnp.sum(acc, axis=1, keepdims=True)
        idx_ref[0, pl.ds(base, TOPK_SG), :] = jnp.minimum(tok, jnp.float32(T - 1)).astype(jnp.int32)
        return 0

    lax.fori_loop(0, cap // TOPK_SG, group, 0)


def topk_tokens(afft, cap):
    T = afft.shape[1]
    E = N_EXPERTS
    return pl.pallas_call(
        functools.partial(_topk_kernel, cap=cap, T=T),
        grid=(E,),
        in_specs=[pl.BlockSpec((E, T), lambda e: (0, 0))],
        out_specs=pl.BlockSpec((1, cap, 1), lambda e: (e, 0, 0)),
        out_shape=jax.ShapeDtypeStruct((E, cap, 1), jnp.int32),
        scratch_shapes=[pltpu.VMEM((E, T), jnp.int32), pltpu.VMEM((E, T), F32), pltpu.VMEM((1, T), F32)],
        compiler_params=_params(("arbitrary",)),
        name="topk_tokens",
    )(afft)


def _row_copy(src_hbm, dst_hbm_or_vmem, src_row, dst_row, sem):
    return pltpu.make_async_copy(src_hbm.at[pl.ds(src_row, 1), :], dst_hbm_or_vmem.at[pl.ds(dst_row, 1), :], sem)


def _expert_kernel(idx_hbm, hx_hbm, wg_ref, wu_ref, wd_ref, xin_hbm, out_hbm,
                   idx_s, xe, xb, acc, rows, sem_i, sem_g, sem_r, sem_w, *, rm, rc, nf):
    del xin_hbm
    e = pl.program_id(0)
    r = pl.program_id(1)
    f = pl.program_id(2)

    @pl.when(f == 0)
    def _():
        cp = pltpu.make_async_copy(idx_hbm.at[e, r], idx_s, sem_i)
        cp.start()
        cp.wait()

        def issue(s, _):
            _row_copy(hx_hbm, xe, idx_s[0, s], s, sem_g).start()
            return 0

        lax.fori_loop(0, rm, issue, 0)

        def drain(s, _):
            _row_copy(hx_hbm, xe, idx_s[0, s], s, sem_g).wait()
            return 0

        lax.fori_loop(0, rm, drain, 0)
        xb[...] = xe[:, :D_MODEL].astype(BF16)
        acc[...] = jnp.zeros_like(acc)

    xv = xb[...]
    hg = jnp.dot(xv, wg_ref[0].astype(BF16), preferred_element_type=F32)
    hu = jnp.dot(xv, wu_ref[0].astype(BF16), preferred_element_type=F32)
    hid = (hg * jax.nn.sigmoid(hg)) * hu
    acc[...] += jnp.dot(hid.astype(BF16), wd_ref[0].astype(BF16), preferred_element_type=F32)

    @pl.when(f == nf - 1)
    def _():
        lane = lax.broadcasted_iota(jnp.int32, (rm, LANES), 1)
        gate = jnp.sum(jnp.where(lane == e, xe[:, D_MODEL:], 0.0), axis=1, keepdims=True)
        acc[...] = acc[...] * gate

        def chunk(c, _):
            base = pl.multiple_of(c * rc, rc)

            def g_issue(s, _):
                _row_copy(out_hbm, rows, idx_s[0, base + s], s, sem_r).start()
                return 0

            def g_drain(s, _):
                _row_copy(out_hbm, rows, idx_s[0, base + s], s, sem_r).wait()
                return 0

            lax.fori_loop(0, rc, g_issue, 0)
            lax.fori_loop(0, rc, g_drain, 0)
            rows[...] = rows[...] + acc[pl.ds(base, rc), :]

            def s_issue(s, _):
                _row_copy(rows, out_hbm, s, idx_s[0, base + s], sem_w).start()
                return 0

            def s_drain(s, _):
                _row_copy(rows, out_hbm, s, idx_s[0, base + s], sem_w).wait()
                return 0

            lax.fori_loop(0, rc, s_issue, 0)
            lax.fori_loop(0, rc, s_drain, 0)
            return 0

        lax.fori_loop(0, rm // rc, chunk, 0)


def expert_ffn_scatter(idx, hx, wg, wu, wd, x, rm=1024, rc=256, tf=256):
    E, cap = idx.shape
    T = x.shape[0]
    rm = min(rm, cap)
    rc = min(rc, rm)
    nr = cap // rm
    nf = EXPERT_FF // tf
    idx4 = idx.reshape(E, nr, 1, rm)
    return pl.pallas_call(
        functools.partial(_expert_kernel, rm=rm, rc=rc, nf=nf),
        grid=(E, nr, nf),
        in_specs=[pl.BlockSpec(memory_space=pl.ANY),
                  pl.BlockSpec(memory_space=pl.ANY),
                  pl.BlockSpec((1, D_MODEL, tf), lambda e, r, f: (e, 0, f)),
                  pl.BlockSpec((1, D_MODEL, tf), lambda e, r, f: (e, 0, f)),
                  pl.BlockSpec((1, tf, D_MODEL), lambda e, r, f: (e, f, 0)),
                  pl.BlockSpec(memory_space=pl.ANY)],
        out_specs=pl.BlockSpec(memory_space=pl.ANY),
        out_shape=jax.ShapeDtypeStruct((T, D_MODEL), F32),
        scratch_shapes=[pltpu.SMEM((1, rm), jnp.int32),
                        pltpu.VMEM((rm, D_MODEL + LANES), F32),
                        pltpu.VMEM((rm, D_MODEL), BF16),
                        pltpu.VMEM((rm, D_MODEL), F32),
                        pltpu.VMEM((rc, D_MODEL), F32),
                        pltpu.SemaphoreType.DMA, pltpu.SemaphoreType.DMA,
                        pltpu.SemaphoreType.DMA, pltpu.SemaphoreType.DMA],
        input_output_aliases={5: 0},
        compiler_params=_params(("arbitrary", "arbitrary", "arbitrary")),
        name="expert_ffn",
    )(idx4, hx, wg, wu, wd, x)


def _rmsnorm(x, g):
    xf = x.astype(F32)
    y = xf * lax.rsqrt(jnp.mean(xf * xf, axis=-1, keepdims=True) + RMS_EPS)
    return (y * g.astype(F32)).astype(x.dtype)


def _rev(t):
    return jnp.flip(t, axis=1)


def _centred_dwconv(u, w, b):
    K = w.shape[0]
    pad = K // 2
    L = u.shape[1]
    up = jnp.pad(u, ((0, 0), (pad, pad), (0, 0)))
    return sum(up[:, j:j + L] * w[j] for j in range(K)) + b


def _masked_exp(diff, mask):
    return jnp.where(mask, jnp.exp(jnp.where(mask, diff, 0.0)), 0.0)


def _ssd_scan(xdt_in, da_in, bm, cm):
    B, L, H, P = xdt_in.shape
    R = H // SSD_G
    C = SSD_CHUNK
    n = L // C

    def chunks(t):
        t = t.astype(F32).reshape((B, n, C) + t.shape[2:])
        return jnp.moveaxis(t, 1, 0)

    xs = (chunks(xdt_in.reshape(B, L, SSD_G, R, P)), chunks(da_in.reshape(B, L, SSD_G, R)), chunks(bm), chunks(cm))
    causal = jnp.tril(jnp.ones((C, C), bool))[None, :, :, None, None]

    def step(S, inp):
        xc, ac, bc, cc = inp
        cs = jnp.cumsum(ac, axis=1)
        seg = _masked_exp(cs[:, :, None] - cs[:, None, :], causal)
        cb = jnp.einsum('bign,bjgn->bijg', cc, bc)
        y = jnp.einsum('bijg,bijgr,bjgrp->bigrp', cb, seg, xc)
        y = y + jnp.einsum('bign,bgrpn->bigrp', cc, S) * jnp.exp(cs)[..., None]
        last = cs[:, -1]
        w = jnp.exp(last[:, None] - cs)
        S = jnp.exp(last)[..., None, None] * S + jnp.einsum('bjgn,bjgr,bjgrp->bgrpn', bc, w, xc)
        return S, y

    S0 = jnp.zeros((B, SSD_G, R, P, SSD_N), F32)
    _, ys = lax.scan(step, S0, xs)
    return jnp.moveaxis(ys, 0, 1).reshape(B, L, H, P)


def _ssd_mixer(z, xbc, dt_raw, conv_w, conv_b, dt_bias, a_log, d_skip, norm_w):
    B, L, _ = z.shape
    xbc = jax.nn.silu(_centred_dwconv(xbc, conv_w, conv_b))
    xs, bm, cm = jnp.split(xbc, [SSD_W, SSD_W + SSD_G * SSD_N], axis=-1)
    xs = xs.reshape(B, L, SSD_H, SSD_HEAD_DIM).astype(F32)
    bm = bm.reshape(B, L, SSD_G, SSD_N)
    cm = cm.reshape(B, L, SSD_G, SSD_N)
    dt = jax.nn.softplus(dt_raw.reshape(B, L, 2, SSD_H).astype(F32) + dt_bias.astype(F32))
    a = -jnp.exp(a_log.astype(F32))
    dt_f, dt_b = dt[:, :, 0], dt[:, :, 1]
    y_f = _ssd_scan(xs * dt_f[..., None], dt_f * a[0], bm, cm)
    y_b = _rev(_ssd_scan(_rev(xs * dt_b[..., None]), _rev(dt_b * a[1]), _rev(bm), _rev(cm)))
    y = y_f + y_b + xs * d_skip.astype(F32)[:, None]
    y = y.reshape(B, L, SSD_W) * jax.nn.silu(z.astype(F32))
    y = _rmsnorm(y.reshape(B, L, SSD_G, SSD_W // SSD_G), norm_w.reshape(SSD_G, SSD_W // SSD_G))
    return y.reshape(B, L, SSD_W)


def _rope(t, pos):
    half = ATTN_HEAD_DIM // 2
    inv = ROPE_THETA ** (-jnp.arange(half, dtype=F32) / half)
    ang = pos[:, None] * inv[None]
    cos = jnp.cos(ang)[None, :, None]
    sin = jnp.sin(ang)[None, :, None]
    t = t.astype(F32)
    t1, t2 = t[..., :half], t[..., half:]
    return jnp.concatenate([t1 * cos - t2 * sin, t2 * cos + t1 * sin], axis=-1)


def _banded_attention(q, k, v, sink):
    B, L, _, Dh = q.shape
    W = ATTN_WINDOW
    nb = L // W
    qb = q.reshape(B, nb, W, ATTN_HKV, ATTN_GROUP, Dh)

    def band(t):
        tp = jnp.pad(t.astype(F32), ((0, 0), (W, W), (0, 0), (0, 0))).reshape(B, nb + 2, W, ATTN_HKV, Dh)
        return jnp.concatenate([tp[:, :-2], tp[:, 1:-1], tp[:, 2:]], axis=2)

    kb, vb = band(k), band(v)
    s = jnp.einsum('bnqhgd,bnkhd->bnhgqk', qb, kb) * (1.0 / math.sqrt(Dh))
    qpos = jnp.arange(nb)[:, None] * W + jnp.arange(W)[None]
    kpos = jnp.arange(nb)[:, None] * W - W + jnp.arange(3 * W)[None]
    valid = ((jnp.abs(qpos[:, :, None] - kpos[:, None, :]) <= W)
             & (kpos >= 0)[:, None, :] & (kpos < L)[:, None, :])
    valid = valid[None, :, None, None]
    s = jnp.where(valid, s, MASK_NEG)
    sk = sink.astype(F32).reshape(ATTN_HKV, ATTN_GROUP)[None, None, :, :, None, None]
    m = jnp.maximum(jnp.max(s, axis=-1, keepdims=True), sk)
    p = jnp.where(valid, jnp.exp(s - m), 0.0)
    denom = jnp.sum(p, axis=-1, keepdims=True) + jnp.exp(sk - m)
    o = jnp.einsum('bnhgqk,bnkhd->bnqhgd', p / denom, vb)
    return o.reshape(B, L, ATTN_HQ * Dh)


def _attention_mixer(q, k, v, sink):
    B, L, _ = q.shape
    q = q.reshape(B, L, ATTN_HQ, ATTN_HEAD_DIM)
    k = k.reshape(B, L, ATTN_HKV, ATTN_HEAD_DIM)
    v = v.reshape(B, L, ATTN_HKV, ATTN_HEAD_DIM)
    pos = jnp.arange(L, dtype=F32)
    return _banded_attention(_rope(q, pos), _rope(k, pos), v, sink)


def _hgrn2_scan(q, lf, k, v):
    B, L, H, K = q.shape
    V = v.shape[-1]
    C = HG_CHUNK
    n = L // C

    def chunks(t):
        return t.reshape(B, n, C, H, t.shape[-1]).transpose(1, 0, 3, 2, 4)

    causal = jnp.tril(jnp.ones((C, C), bool))[:, :, None]

    def step(S, inp):
        qc, lfc, kc, vc = inp
        b = jnp.cumsum(lfc, axis=2)
        dec = _masked_exp(b[:, :, :, None] - b[:, :, None], causal)
        att = jnp.einsum('bhik,bhijk,bhjk->bhij', qc, dec, kc)
        o = jnp.einsum('bhij,bhjv->bhiv', att, vc) + jnp.einsum('bhik,bhkv->bhiv', qc * jnp.exp(b), S)
        last = b[:, :, -1]
        S = jnp.exp(last)[..., None] * S + jnp.einsum('bhjk,bhjv->bhkv', kc * jnp.exp(last[:, :, None] - b), vc)
        return S, o

    S0 = jnp.zeros((B, H, K, V), F32)
    _, os = lax.scan(step, S0, (chunks(q), chunks(lf), chunks(k), chunks(v)))
    return os.transpose(1, 0, 3, 2, 4).reshape(B, L, H, V)


def _hgrn2_mixer(seg, lb_f, lb_b, norm_w):
    B, L, _ = seg.shape
    q, ff, fb, i, g = jnp.split(seg, 5, axis=-1)
    shp = (B, L, HG_H, HG_HEAD_DIM)
    q, ff, fb, i = (t.reshape(shp).astype(F32) for t in (q, ff, fb, i))

    def gates(raw, lb):
        lb = lb.astype(F32).reshape(HG_H, HG_HEAD_DIM)
        lf = jnp.logaddexp(jnp.log(jnp.maximum(lb, LB_FLOOR)), jnp.log1p(-lb) + jax.nn.log_sigmoid(raw))
        return lf, -jnp.expm1(lf)

    lf_f, k_f = gates(ff, lb_f)
    lf_b, k_b = gates(fb, lb_b)
    o = _hgrn2_scan(q, lf_f, k_f, i) + _rev(_hgrn2_scan(_rev(q), _rev(lf_b), _rev(k_b), _rev(i)))
    o = _rmsnorm(o, norm_w) * jax.nn.silu(g.reshape(shp).astype(F32))
    return o.reshape(B, L, HG_W)


def _hyena_filters(L, w_in, b_in, w_mid, b_mid, freq, w_out):
    t = jnp.linspace(0.0, 1.0, L, dtype=F32)[:, None]
    w = 2.0 * math.pi * jnp.arange(L, dtype=F32)[:, None] / L
    bands = jnp.linspace(1e-4, HY_BANDS - 1, HY_BANDS, dtype=F32)[None]
    z = jnp.concatenate([t, jnp.cos(bands * w), -jnp.sin(bands * w)], axis=-1)
    h = jnp.sin(freq[0] * (z @ w_in + b_in))
    for m in range(HY_INNER):
        h = jnp.sin(freq[m + 1] * (h @ w_mid[m] + b_mid[m]))
    h = (h @ w_out).astype(F32)
    max_decay = math.log(HY_DECAY_TARGET) / HY_FAST_DECAY
    min_decay = math.log(HY_DECAY_TARGET) / HY_SLOW_DECAY
    deltas = jnp.abs(jnp.linspace(min_decay, max_decay, HY_W, dtype=F32))
    h = h * jnp.exp(-t * jnp.tile(deltas, 2)[None])
    return h[:, :HY_W], h[:, HY_W:]


def _two_sided_fftconv(u, h_fwd, h_bwd, bias):
    L = u.shape[1]
    zero = jnp.zeros((1, h_fwd.shape[1]), F32)
    buf = jnp.concatenate([h_fwd[:1] + h_bwd[:1], h_fwd[1:], zero, jnp.flip(h_bwd[1:], axis=0)], axis=0)
    hf = jnp.fft.rfft(buf, axis=0)
    uf = jnp.fft.rfft(u.astype(F32), n=2 * L, axis=1)
    y = jnp.fft.irfft(uf * hf[None], n=2 * L, axis=1)[:, :L]
    return y + u.astype(F32) * bias.astype(F32)


def _hyena_mixer(seg, conv_w, conv_b, f_w_in, f_b_in, f_w_mid, f_b_mid, f_freq, f_w_out, bias):
    L = seg.shape[1]
    u = _centred_dwconv(seg, conv_w, conv_b)
    x0, x1, v = jnp.split(u, 3, axis=-1)
    h_fwd, h_bwd = _hyena_filters(L, f_w_in, f_b_in, f_w_mid, f_b_mid, f_freq, f_w_out)
    return _two_sided_fftconv(v * x1, h_fwd, h_bwd, bias) * x0.astype(F32)


def _prep_layer_weights(p, l):
    w = p['w_in'][l]
    o = 0
    gate = w[:, o:o + GATE_COLS]; o += GATE_COLS
    ssd = w[:, o:o + SSD_COLS]; o += SSD_COLS
    attn = w[:, o:o + ATTN_COLS]; o += ATTN_COLS
    hg = w[:, o:o + HG_COLS]; o += HG_COLS
    hy = w[:, o:o + HY_COLS]
    z, xbc, dt = ssd[:, :SSD_W], ssd[:, SSD_W:SSD_W + SSD_CONV_CH], ssd[:, SSD_W + SSD_CONV_CH:]
    q, kv = attn[:, :ATTN_HQ * ATTN_HEAD_DIM], attn[:, ATTN_HQ * ATTN_HEAD_DIM:]
    cols = [gate, hg, hy, z, q, xbc, kv, dt]
    used = sum(c.shape[1] for c in cols)
    cols.append(jnp.zeros((D_MODEL, N_PROJ - used), F32))
    w_all = jnp.concatenate(cols, axis=1).astype(BF16)
    wr = jnp.pad(p['w_router'][l], ((0, 0), (0, LANES - N_EXPERTS))).astype(BF16)
    return dict(w_all=w_all, wb=p['w_branch'][l].astype(BF16), w_out=p['w_out'][l].astype(BF16), wr=wr)


def _trunk(x, p, prep):
    B, L, _ = x.shape
    T = B * L
    cap = EC_CAPACITY * T // N_EXPERTS
    lbw = jax.nn.softmax(p['hgrn_lb_logits'].astype(F32), axis=1)
    lb_all = jnp.cumsum(lbw, axis=1) - lbw[:, :1]
    x = x.reshape(T, D_MODEL)
    for l in range(DEPTH):
        w = prep[l]
        proj = norm_proj(x, p['norm_mix'][l].reshape(1, D_MODEL), w['w_all'])

        def seg(off, width):
            return proj[:, off:off + width].reshape(B, L, width)

        y_a = _ssd_mixer(seg(OFF_Z, SSD_W), seg(OFF_XBC, SSD_CONV_CH), seg(OFF_DT, 2 * SSD_H),
                         p['ssd_conv_w'][l], p['ssd_conv_b'][l], p['ssd_dt_bias'][l],
                         p['ssd_a_log'][l], p['ssd_d'][l], p['ssd_norm'][l])
        y_b = _attention_mixer(seg(OFF_Q, ATTN_HQ * ATTN_HEAD_DIM), seg(OFF_KV, ATTN_HKV * ATTN_HEAD_DIM),
                               seg(OFF_KV + ATTN_HKV * ATTN_HEAD_DIM, ATTN_HKV * ATTN_HEAD_DIM), p['attn_sink'][l])
        y_c = _hgrn2_mixer(seg(OFF_HG, HG_COLS), lb_all[0, l], lb_all[1, l], p['hgrn_norm'][l])
        y_d = _hyena_mixer(seg(OFF_HY, HY_COLS), p['hy_conv_w'][l], p['hy_conv_b'][l], p['hy_filt_w_in'][l],
                           p['hy_filt_b_in'][l], p['hy_filt_w_mid'][l], p['hy_filt_b_mid'][l],
                           p['hy_filt_freq'][l], p['hy_filt_w_out'][l], p['hy_bias'][l])
        ys = [t.reshape(T, BR_W).astype(F32) for t in (y_a, y_b, y_c, y_d)]
        merged = merge_branches(ys, w['wb'], proj)
        x = mm_residual(merged, w['w_out'], x)
        hx, afft = router(x, p['norm_ffn'][l].reshape(1, D_MODEL), w['wr'])
        idx = topk_tokens(afft, cap).reshape(N_EXPERTS, cap)
        x = expert_ffn_scatter(idx, hx, p['w_gate'][l], p['w_up'][l], p['w_down'][l], x)
    return final_norm(x, p['norm_final'].reshape(1, D_MODEL)).reshape(B, L, D_MODEL)


def kernel(x_prompt, x_sample, norm_mix, w_in, ssd_conv_w, ssd_conv_b, ssd_dt_bias, ssd_a_log, ssd_d, ssd_norm, attn_sink, hgrn_lb_logits, hgrn_norm, hy_conv_w, hy_conv_b, hy_filt_w_in, hy_filt_b_in, hy_filt_w_mid, hy_filt_b_mid, hy_filt_freq, hy_filt_w_out, hy_bias, w_branch, w_out, norm_ffn, w_router, w_gate, w_up, w_down, norm_final):
    p = {'norm_mix': norm_mix, 'w_in': w_in, 'ssd_conv_w': ssd_conv_w, 'ssd_conv_b': ssd_conv_b,
         'ssd_dt_bias': ssd_dt_bias, 'ssd_a_log': ssd_a_log, 'ssd_d': ssd_d, 'ssd_norm': ssd_norm,
         'attn_sink': attn_sink, 'hgrn_lb_logits': hgrn_lb_logits, 'hgrn_norm': hgrn_norm,
         'hy_conv_w': hy_conv_w, 'hy_conv_b': hy_conv_b, 'hy_filt_w_in': hy_filt_w_in,
         'hy_filt_b_in': hy_filt_b_in, 'hy_filt_w_mid': hy_filt_w_mid, 'hy_filt_b_mid': hy_filt_b_mid,
         'hy_filt_freq': hy_filt_freq, 'hy_filt_w_out': hy_filt_w_out, 'hy_bias': hy_bias,
         'w_branch': w_branch, 'w_out': w_out, 'norm_ffn': norm_ffn, 'w_router': w_router,
         'w_gate': w_gate, 'w_up': w_up, 'w_down': w_down, 'norm_final': norm_final}
    prep = [_prep_layer_weights(p, l) for l in range(DEPTH)]
    return (_trunk(x_prompt, p, prep), _trunk(x_sample, p, prep))
```
